```python
import numpy as np
import jax
import jax.numpy as jnp
from jax import lax

D_MODEL = 1024
BATCH = 8
SEQ = 2048
DEPTH = 4
DEC_BATCH = 32
DEC_SEQ = 4
PAST_LEN = 8192
PAGE_SIZE = 128

N_META = 16
N_HEADS = 16
HEAD_DIM = 64
N_KV_HEADS = 4
IDX_HEADS = 8
IDX_DIM = 64
TOPK_MAX = 256
Q_BLOCK = 128
ROPE_THETA = 10000.0
QKV_DIM = N_HEADS * HEAD_DIM + 2 * N_KV_HEADS * HEAD_DIM + IDX_HEADS * IDX_DIM + IDX_DIM + IDX_HEADS
D_RNN = D_MODEL
RG_BLOCKS = 8
CONV_WIDTH = 4
LRU_C = 8.0
N_EXPERTS = 32
TOP_K = 4
D_FF = D_MODEL
SWIGLU_ALPHA = 1.702
SWIGLU_LIMIT = 7.0
MOE_BLOCK = 128
DN_ALPHA = (2.0 * DEPTH) ** 0.25
DN_BETA = (8.0 * DEPTH) ** -0.25
LN_EPS = 1e-5
N_ATT_LAYERS = DEPTH // 2
N_RG_LAYERS = (DEPTH + 1) // 2

kernel_name = 'hybrid_rglru_dsa_moe_step'


def layer_norm(x, g, b):
    xf = x.astype(jnp.float32)
    mu = jnp.mean(xf, axis=-1, keepdims=True)
    var = jnp.mean(jnp.square(xf - mu), axis=-1, keepdims=True)
    return ((xf - mu) * lax.rsqrt(var + LN_EPS) * g.astype(jnp.float32) + b.astype(jnp.float32)).astype(x.dtype)


def rope(x, pos):
    half = x.shape[-1] // 2
    inv_freq = ROPE_THETA ** (-jnp.arange(half, dtype=jnp.float32) / half)
    ang = pos.astype(jnp.float32)[:, None] * inv_freq[None, :]
    cos = jnp.cos(ang)[:, None, :]
    sin = jnp.sin(ang)[:, None, :]
    xf = x.astype(jnp.float32)
    x1, x2 = xf[..., :half], xf[..., half:]
    return jnp.concatenate([x1 * cos - x2 * sin, x2 * cos + x1 * sin], axis=-1).astype(x.dtype)


def _lin_combine(left, right):
    a1, b1 = left
    a2, b2 = right
    return a1 * a2, a2 * b1 + b2


def rglru_mixer(x, pos, conv_buf, h0, w_in, conv_w, conv_b, w_a, b_a, w_i, b_i, lam, w_out):
    B, T, _ = x.shape
    gu = x @ w_in
    gate, u = gu[..., :D_RNN], gu[..., D_RNN:]
    upad = jnp.concatenate([conv_buf.astype(u.dtype), u], axis=1)
    uc = conv_b
    for j in range(CONV_WIDTH):
        uc = uc + conv_w[j] * upad[:, j:j + T]
    ub = uc.reshape(B, T, RG_BLOCKS, D_RNN // RG_BLOCKS)
    r = jax.nn.sigmoid(jnp.einsum('btnc,ncd->btnd', ub, w_a).reshape(B, T, D_RNN) + b_a)
    gi = jax.nn.sigmoid(jnp.einsum('btnc,ncd->btnd', ub, w_i).reshape(B, T, D_RNN) + b_i)
    log_a = -LRU_C * r.astype(jnp.float32) * jax.nn.softplus(-lam.astype(jnp.float32))
    a = jnp.exp(log_a)
    mult = jnp.where((pos == 0)[None, :, None], 1.0, jnp.sqrt(-jnp.expm1(2.0 * log_a)))
    bx = uc.astype(jnp.float32) * gi.astype(jnp.float32) * mult
    a_cum, b_cum = lax.associative_scan(_lin_combine, (a, bx), axis=1)
    h = a_cum * h0.astype(jnp.float32)[:, None, :] + b_cum
    y = (h.astype(x.dtype) * jax.nn.gelu(gate)) @ w_out
    return y, upad[:, -(CONV_WIDTH - 1):], h[:, -1].astype(h0.dtype)


def dsa_project(x, pos, w_qkv):
    B, T, _ = x.shape
    p = x @ w_qkv
    sizes = (N_HEADS * HEAD_DIM, N_KV_HEADS * HEAD_DIM, N_KV_HEADS * HEAD_DIM, IDX_HEADS * IDX_DIM, IDX_DIM)
    cuts = [int(c) for c in np.cumsum(sizes)]
    q, k, v, qi, ki, wi = jnp.split(p, cuts, axis=-1)
    q = rope(q.reshape(B, T, N_HEADS, HEAD_DIM), pos)
    k = rope(k.reshape(B, T, N_KV_HEADS, HEAD_DIM), pos)
    v = v.reshape(B, T, N_KV_HEADS, HEAD_DIM)
    qi = rope(qi.reshape(B, T, IDX_HEADS, IDX_DIM), pos)
    ki = rope(ki.reshape(B, T, 1, IDX_DIM), pos)[:, :, 0]
    return q, k, v, qi, ki, wi


def indexer_scores(qi, wi, ki, pos_q, pos_k):
    dots = jnp.einsum('bqhd,bsd->bqhs', qi, ki, preferred_element_type=jnp.float32) * (IDX_DIM ** -0.5)
    s = jnp.einsum('bqhs,bqh->bqs', jax.nn.relu(dots), wi.astype(jnp.float32) * (IDX_HEADS ** -0.5))
    return jnp.where(pos_k[None, None, :] <= pos_q[None, :, None], s, -jnp.inf)


def sparse_attend(q, kg, vg, valid):
    B, Q = q.shape[:2]
    qg = q.reshape(B, Q, N_KV_HEADS, N_HEADS // N_KV_HEADS, HEAD_DIM)
    logits = jnp.einsum('bqngd,bqknd->bqngk', qg, kg, preferred_element_type=jnp.float32) * (HEAD_DIM ** -0.5)
    logits = jnp.where(valid[:, :, None, None, :], logits, -jnp.inf)
    p = jax.nn.softmax(logits, axis=-1)
    o = jnp.einsum('bqngk,bqknd->bqngd', p.astype(vg.dtype), vg)
    return o.reshape(B, Q, N_HEADS * HEAD_DIM)


def dsa_prompt(x, pos, w_qkv, w_o):
    B, L, _ = x.shape
    q, k, v, qi, ki, wi = dsa_project(x, pos, w_qkv)
    topk = min(TOPK_MAX, L // 4)
    nqb = -(-L // Q_BLOCK)
    pad = nqb * Q_BLOCK - L

    def blocks(t):
        t = jnp.pad(t, [(0, 0), (0, pad)] + [(0, 0)] * (t.ndim - 2))
        return jnp.moveaxis(t.reshape((B, nqb, Q_BLOCK) + t.shape[2:]), 1, 0)

    pos_blocks = jnp.arange(nqb * Q_BLOCK, dtype=jnp.int32).reshape(nqb, Q_BLOCK)
    bidx = jnp.arange(B)[:, None, None]

    def one_block(args):
        qb, qib, wib, pq = args
        s = indexer_scores(qib, wib, ki, pq, pos)
        _, sel = lax.top_k(s, topk)
        valid = sel <= pq[None, :, None]
        return sparse_attend(qb, k[bidx, sel], v[bidx, sel], valid)

    o = lax.map(one_block, (blocks(q), blocks(qi), blocks(wi), pos_blocks))
    o = jnp.moveaxis(o, 0, 1).reshape(B, nqb * Q_BLOCK, N_HEADS * HEAD_DIM)[:, :L]
    return o @ w_o, k, v, ki


def dsa_sample(x, pos, layer, cache_k, cache_v, cache_idx_k, page_table, w_qkv, w_o):
    B, T, _ = x.shape
    past = page_table.shape[1] * PAGE_SIZE
    q, k, v, qi, ki, wi = dsa_project(x, pos, w_qkv)
    ki_past = cache_idx_k[layer, page_table].reshape(B, past, IDX_DIM)
    ki_all = jnp.concatenate([ki_past.astype(ki.dtype), ki], axis=1)
    pos_k = jnp.arange(past + T, dtype=jnp.int32)
    s = indexer_scores(qi, wi, ki_all, pos, pos_k)
    topk = min(TOPK_MAX, (past + T) // 4)
    _, sel = lax.top_k(s, topk)
    valid = sel <= pos[None, :, None]
    bidx = jnp.arange(B)[:, None, None]
    in_past = (sel < past)[..., None, None]
    sp = jnp.minimum(sel, past - 1)
    phys = page_table[bidx, sp // PAGE_SIZE]
    off = sp % PAGE_SIZE
    sn = jnp.clip(sel - past, 0, T - 1)
    kg = jnp.where(in_past, cache_k[layer, phys, off].astype(k.dtype), k[bidx, sn])
    vg = jnp.where(in_past, cache_v[layer, phys, off].astype(v.dtype), v[bidx, sn])
    o = sparse_attend(q, kg, vg, valid)
    return o @ w_o, k, v, ki


def moe_ffn(x, w_r, b_r, w_gu, b_gu, w_dn, b_dn):
    shp = x.shape
    xf = x.reshape(-1, D_MODEL)
    n = xf.shape[0]
    logits = xf.astype(jnp.float32) @ w_r.astype(jnp.float32) + b_r.astype(jnp.float32)
    top_val, top_idx = lax.top_k(logits, TOP_K)
    gates = jax.nn.softmax(top_val, axis=-1)
    n_assign = n * TOP_K
    expert = top_idx.reshape(-1).astype(jnp.int32)
    token = jnp.repeat(jnp.arange(n, dtype=jnp.int32), TOP_K)
    gate = gates.reshape(-1)
    counts = jnp.bincount(expert, length=N_EXPERTS)
    order = jnp.argsort(expert, stable=True)
    se = expert[order]
    start = jnp.cumsum(counts) - counts
    rank = jnp.arange(n_assign, dtype=jnp.int32) - start[se]
    padded = (counts + MOE_BLOCK - 1) // MOE_BLOCK * MOE_BLOCK
    pend = jnp.cumsum(padded)
    dest = pend[se] - padded[se] + rank
    n_blocks = -(-n_assign // MOE_BLOCK) + N_EXPERTS
    slots = n_blocks * MOE_BLOCK
    tok_buf = jnp.full((slots,), n, jnp.int32).at[dest].set(token[order])
    gate_buf = jnp.zeros((slots,), jnp.float32).at[dest].set(gate[order])
    block_expert = jnp.minimum(
        jnp.searchsorted(pend, jnp.arange(n_blocks, dtype=pend.dtype) * MOE_BLOCK, side='right'), N_EXPERTS - 1)
    x_buf = jnp.concatenate([xf, jnp.zeros((1, D_MODEL), xf.dtype)], axis=0)[tok_buf]
    x_buf = x_buf.reshape(n_blocks, MOE_BLOCK, D_MODEL)

    def expert_block(args):
        xb, e = args
        h = xb @ w_gu[e] + b_gu[e]
        glu = jnp.minimum(h[:, :D_FF], SWIGLU_LIMIT)
        lin = jnp.clip(h[:, D_FF:], -SWIGLU_LIMIT, SWIGLU_LIMIT)
        act = glu * jax.nn.sigmoid(SWIGLU_ALPHA * glu) * (lin + 1.0)
        return act @ w_dn[e] + b_dn[e]

    y_buf = lax.map(expert_block, (x_buf, block_expert)).reshape(slots, D_MODEL)
    y = jax.ops.segment_sum(y_buf.astype(jnp.float32) * gate_buf[:, None], tok_buf, num_segments=n + 1)[:n]
    return y.astype(x.dtype).reshape(shp)


def setup_inputs(seed: int = 0) -> dict:
    key = jax.random.key(seed)
    ks = jax.random.split(key, 32)
    f32 = jnp.float32
    n_pages = PAST_LEN // PAGE_SIZE
    n_pool = (5 * DEC_BATCH * n_pages) // 4
    blk = D_RNN // RG_BLOCKS

    def nrm(k, shape, scale):
        return jax.random.normal(k, shape, f32) * scale

    a0 = jax.random.uniform(ks[0], (N_RG_LAYERS, D_RNN), f32, 0.9, 0.999)
    a_c = a0 ** (1.0 / LRU_C)
    lam = jnp.log(a_c) - jnp.log1p(-a_c)
    page_table = jax.random.permutation(ks[1], n_pool)[:DEC_BATCH * n_pages]
    page_table = page_table.reshape(DEC_BATCH, n_pages).astype(jnp.int32)
    return {
        'x_prompt': nrm(ks[2], (BATCH, SEQ, D_MODEL), 1.0),
        'x_sample': nrm(ks[3], (DEC_BATCH, DEC_SEQ, D_MODEL), 1.0),
        'cache_k': nrm(ks[4], (N_ATT_LAYERS, n_pool, PAGE_SIZE, N_KV_HEADS, HEAD_DIM), 1.0),
        'cache_v': nrm(ks[5], (N_ATT_LAYERS, n_pool, PAGE_SIZE, N_KV_HEADS, HEAD_DIM), 1.0),
        'cache_idx_k': nrm(ks[6], (N_ATT_LAYERS, n_pool, PAGE_SIZE, IDX_DIM), 1.0),
        'state_conv': nrm(ks[7], (N_RG_LAYERS, DEC_BATCH, CONV_WIDTH - 1, D_RNN), 1.0),
        'state_h': nrm(ks[8], (N_RG_LAYERS, DEC_BATCH, D_RNN), 0.5),
        'page_table': page_table,
        'meta_tokens': nrm(ks[9], (N_META, D_MODEL), 1.0),
        'w_in_rg': nrm(ks[10], (N_RG_LAYERS, D_MODEL, 2 * D_RNN), D_MODEL ** -0.5),
        'conv_w_rg': nrm(ks[11], (N_RG_LAYERS, CONV_WIDTH, D_RNN), CONV_WIDTH ** -0.5),
        'conv_b_rg': nrm(ks[12], (N_RG_LAYERS, D_RNN), 0.01),
        'w_a_rg': nrm(ks[13], (N_RG_LAYERS, RG_BLOCKS, blk, blk), blk ** -0.5),
        'b_a_rg': nrm(ks[14], (N_RG_LAYERS, D_RNN), 0.01),
        'w_i_rg': nrm(ks[15], (N_RG_LAYERS, RG_BLOCKS, blk, blk), blk ** -0.5),
        'b_i_rg': nrm(ks[16], (N_RG_LAYERS, D_RNN), 0.01),
        'lam_rg': lam,
        'w_out_rg': nrm(ks[17], (N_RG_LAYERS, D_RNN, D_MODEL), DN_BETA * D_RNN ** -0.5),
        'w_qkv_att': nrm(ks[18], (N_ATT_LAYERS, D_MODEL, QKV_DIM), D_MODEL ** -0.5),
        'w_o_att': nrm(ks[19], (N_ATT_LAYERS, N_HEADS * HEAD_DIM, D_MODEL), DN_BETA * (N_HEADS * HEAD_DIM) ** -0.5),
        'ln_mix_g': 1.0 + nrm(ks[20], (DEPTH, D_MODEL), 0.02),
        'ln_mix_b': nrm(ks[21], (DEPTH, D_MODEL), 0.02),
        'w_router': nrm(ks[22], (DEPTH, D_MODEL, N_EXPERTS), D_MODEL ** -0.5),
        'b_router': nrm(ks[23], (DEPTH, N_EXPERTS), 0.01),
        'w_moe_gu': nrm(ks[24], (DEPTH, N_EXPERTS, D_MODEL, 2 * D_FF), D_MODEL ** -0.5),
        'b_moe_gu': nrm(ks[25], (DEPTH, N_EXPERTS, 2 * D_FF), 0.01),
        'w_moe_dn': nrm(ks[26], (DEPTH, N_EXPERTS, D_FF, D_MODEL), DN_BETA * D_FF ** -0.5),
        'b_moe_dn': nrm(ks[27], (DEPTH, N_EXPERTS, D_MODEL), 0.01),
        'ln_ffn_g': 1.0 + nrm(ks[28], (DEPTH, D_MODEL), 0.02),
        'ln_ffn_b': nrm(ks[29], (DEPTH, D_MODEL), 0.02),
    }


def reference(x_prompt, x_sample, cache_k, cache_v, cache_idx_k, state_conv, state_h, page_table,
              meta_tokens, w_in_rg, conv_w_rg, conv_b_rg, w_a_rg, b_a_rg, w_i_rg, b_i_rg, lam_rg, w_out_rg,
              w_qkv_att, w_o_att, ln_mix_g, ln_mix_b, w_router, b_router, w_moe_gu, b_moe_gu,
              w_moe_dn, b_moe_dn, ln_ffn_g, ln_ffn_b):
    B = x_prompt.shape[0]
    meta = jnp.broadcast_to(meta_tokens[None].astype(x_prompt.dtype), (B, N_META, D_MODEL))
    xp = jnp.concatenate([meta, x_prompt], axis=1)
    xs = x_sample
    pos_p = jnp.arange(xp.shape[1], dtype=jnp.int32)
    pos_s = page_table.shape[1] * PAGE_SIZE + jnp.arange(xs.shape[1], dtype=jnp.int32)
    zero_buf = jnp.zeros((B, CONV_WIDTH - 1, D_RNN), xp.dtype)
    zero_h = jnp.zeros((B, D_RNN), xp.dtype)
    kp_l, vp_l, ip_l, ks_l, vs_l, is_l = [], [], [], [], [], []
    cp_l, hp_l, cs_l, hs_l = [], [], [], []
    for i in range(DEPTH):
        j = i // 2
        if i % 2 == 0:
            rg = (w_in_rg[j], conv_w_rg[j], conv_b_rg[j], w_a_rg[j], b_a_rg[j], w_i_rg[j], b_i_rg[j],
                  lam_rg[j], w_out_rg[j])
            yp, cp, hp = rglru_mixer(xp, pos_p, zero_buf, zero_h, *rg)
            ys, cs, hs = rglru_mixer(xs, pos_s, state_conv[j], state_h[j], *rg)
            cp_l.append(cp)
            hp_l.append(hp)
            cs_l.append(cs)
            hs_l.append(hs)
        else:
            yp, kp, vp, ip = dsa_prompt(xp, pos_p, w_qkv_att[j], w_o_att[j])
            ys, kn, vn, inew = dsa_sample(xs, pos_s, j, cache_k, cache_v, cache_idx_k, page_table,
                                          w_qkv_att[j], w_o_att[j])
            kp_l.append(kp)
            vp_l.append(vp)
            ip_l.append(ip)
            ks_l.append(kn)
            vs_l.append(vn)
            is_l.append(inew)
        xp = layer_norm(DN_ALPHA * xp + yp, ln_mix_g[i], ln_mix_b[i])
        xs = layer_norm(DN_ALPHA * xs + ys, ln_mix_g[i], ln_mix_b[i])
        moe = (w_router[i], b_router[i], w_moe_gu[i], b_moe_gu[i], w_moe_dn[i], b_moe_dn[i])
        xp = layer_norm(DN_ALPHA * xp + moe_ffn(xp, *moe), ln_ffn_g[i], ln_ffn_b[i])
        xs = layer_norm(DN_ALPHA * xs + moe_ffn(xs, *moe), ln_ffn_g[i], ln_ffn_b[i])
    y_prompt = xp[:, N_META:]
    return (y_prompt, xs,
            jnp.stack(kp_l), jnp.stack(vp_l), jnp.stack(ip_l), jnp.stack(cp_l), jnp.stack(hp_l),
            jnp.stack(ks_l), jnp.stack(vs_l), jnp.stack(is_l), jnp.stack(cs_l), jnp.stack(hs_l))
```

```python
import functools

import numpy as np
import jax
import jax.numpy as jnp
from jax import lax
from jax.experimental import pallas as pl
from jax.experimental.pallas import tpu as pltpu

F32 = jnp.float32
BF16 = jnp.bfloat16
I32 = jnp.int32

D_MODEL = 1024
N_META = 16
N_HEADS = 16
HEAD_DIM = 64
N_KV_HEADS = 4
KV_GROUP = N_HEADS // N_KV_HEADS
IDX_HEADS = 8
IDX_DIM = 64
TOPK_MAX = 256
ROPE_THETA = 10000.0
D_RNN = D_MODEL
RG_BLOCKS = 8
RG_BLK = D_RNN // RG_BLOCKS
CONV_WIDTH = 4
LRU_C = 8.0
N_EXPERTS = 32
TOP_K = 4
D_FF = D_MODEL
SWIGLU_ALPHA = 1.702
SWIGLU_LIMIT = 7.0
PAGE_SIZE = 128
DEPTH = 4
DN_ALPHA = (2.0 * DEPTH) ** 0.25
LN_EPS = 1e-5

LANES = 128
SUBLANES = 8
VMEM_LIMIT = 56 * 1024 * 1024

NEG_BIG = -1e30
INT_MIN = -(2 ** 31)
KEY_NEG_INF = int(np.int32(np.uint32(0xFF800000)) ^ np.int32(0x7FFFFFFF))

Q_COLS = N_HEADS * HEAD_DIM
KV_COLS = N_KV_HEADS * HEAD_DIM
QI_COLS = IDX_HEADS * IDX_DIM
QKV_PAD = Q_COLS + 2 * KV_COLS + QI_COLS + LANES


def _cparams(sem):
    return pltpu.CompilerParams(dimension_semantics=sem, vmem_limit_bytes=VMEM_LIMIT)


def _round_up(x, m):
    return (x + m - 1) // m * m


def _mm_kernel(x_ref, w_ref, o_ref):
    o_ref[...] = jnp.dot(x_ref[...].astype(BF16), w_ref[...], preferred_element_type=F32)


def matmul(x, w, tm=512):
    n, k = x.shape
    m = w.shape[1]
    return pl.pallas_call(
        _mm_kernel,
        grid=(n // tm,),
        in_specs=[pl.BlockSpec((tm, k), lambda i: (i, 0)),
                  pl.BlockSpec((k, m), lambda i: (0, 0))],
        out_specs=pl.BlockSpec((tm, m), lambda i: (i, 0)),
        out_shape=jax.ShapeDtypeStruct((n, m), F32),
        compiler_params=_cparams(("parallel",)),
        name="matmul",
    )(x, w)


def _layer_norm(z, g, b):
    mu = jnp.mean(z, axis=-1, keepdims=True)
    zc = z - mu
    var = jnp.mean(zc * zc, axis=-1, keepdims=True)
    return zc * lax.rsqrt(var + LN_EPS) * g + b


def _mm_ln_kernel(a_ref, w_ref, res_ref, g_ref, b_ref, o_ref):
    y = jnp.dot(a_ref[...], w_ref[...], preferred_element_type=F32)
    o_ref[...] = _layer_norm(DN_ALPHA * res_ref[...] + y, g_ref[...], b_ref[...])


def matmul_residual_ln(a, w, res, g, b, tm=512):
    n, k = a.shape
    m = w.shape[1]
    return pl.pallas_call(
        _mm_ln_kernel,
        grid=(n // tm,),
        in_specs=[pl.BlockSpec((tm, k), lambda i: (i, 0)),
                  pl.BlockSpec((k, m), lambda i: (0, 0)),
                  pl.BlockSpec((tm, m), lambda i: (i, 0)),
                  pl.BlockSpec((1, m), lambda i: (0, 0)),
                  pl.BlockSpec((1, m), lambda i: (0, 0))],
        out_specs=pl.BlockSpec((tm, m), lambda i: (i, 0)),
        out_shape=jax.ShapeDtypeStruct((n, m), F32),
        compiler_params=_cparams(("parallel",)),
        name="matmul_residual_ln",
    )(a, w, res, g.reshape(1, m), b.reshape(1, m))


def _rope_block(blk, c, s, first_half):
    partner = jnp.where(first_half, pltpu.roll(blk, LANES - HEAD_DIM // 2, 1), pltpu.roll(blk, HEAD_DIM // 2, 1))
    return blk * c + partner * s


def _qkv_kernel(x_ref, w_ref, c_ref, s_ref, c2_ref, s2_ref,
                q_ref, k_ref, v_ref, kb_ref, vb_ref, qi_ref, kiw_ref, kib_ref):
    tm = x_ref.shape[0]
    p = jnp.dot(x_ref[...].astype(BF16), w_ref[...], preferred_element_type=F32)
    c = c_ref[...]
    s = s_ref[...]
    lane = lax.broadcasted_iota(I32, (tm, LANES), 1)
    first_half = (lane & (HEAD_DIM // 2)) == 0
    inv_sqrt_d = HEAD_DIM ** -0.5
    for j in range(Q_COLS // LANES):
        blk = _rope_block(p[:, j * LANES:(j + 1) * LANES], c, s, first_half)
        q_ref[:, j * LANES:(j + 1) * LANES] = (blk * inv_sqrt_d).astype(BF16)
    off = Q_COLS
    for j in range(KV_COLS // LANES):
        blk = _rope_block(p[:, off + j * LANES:off + (j + 1) * LANES], c, s, first_half)
        k_ref[:, j * LANES:(j + 1) * LANES] = blk
        kb_ref[:, j * LANES:(j + 1) * LANES] = blk.astype(BF16)
    off += KV_COLS
    vv = p[:, off:off + KV_COLS]
    v_ref[...] = vv
    vb_ref[...] = vv.astype(BF16)
    off += KV_COLS
    for j in range(QI_COLS // LANES):
        blk = _rope_block(p[:, off + j * LANES:off + (j + 1) * LANES], c, s, first_half)
        qi_ref[:, j * LANES:(j + 1) * LANES] = (blk * (IDX_DIM ** -0.5)).astype(BF16)
    off += QI_COLS
    blk = _rope_block(p[:, off:off + LANES], c2_ref[...], s2_ref[...], first_half)
    kiw_ref[...] = blk
    kib_ref[...] = blk.astype(BF16)


def qkv_project(x, w_pad, c, s, c2, s2, tm=512):
    n, k = x.shape
    row = lambda width: pl.BlockSpec((tm, width), lambda i: (i, 0))
    outs = [(Q_COLS, BF16), (KV_COLS, F32), (KV_COLS, F32), (KV_COLS, BF16), (KV_COLS, BF16),
            (QI_COLS, BF16), (LANES, F32), (LANES, BF16)]
    return pl.pallas_call(
        _qkv_kernel,
        grid=(n // tm,),
        in_specs=[row(k), pl.BlockSpec((k, QKV_PAD), lambda i: (0, 0)),
                  row(LANES), row(LANES), row(LANES), row(LANES)],
        out_specs=[row(wd) for wd, _ in outs],
        out_shape=[jax.ShapeDtypeStruct((n, wd), dt) for wd, dt in outs],
        compiler_params=_cparams(("parallel",)),
        name="qkv_project",
    )(x, w_pad, c, s, c2, s2)


def _gelu_tanh(x):
    return 0.5 * x * (1.0 + jnp.tanh(np.sqrt(2.0 / np.pi) * (x + 0.044715 * (x * x * x))))


def _block_diag(ucb, w_ref):
    return jnp.concatenate(
        [jnp.dot(ucb[:, n * RG_BLK:(n + 1) * RG_BLK], w_ref[n], preferred_element_type=F32)
         for n in range(RG_BLOCKS)], axis=1)


def _rglru_gates(uc, wa_ref, ba_ref, wi_ref, bi_ref, lam_ref, reset_mask):
    ucb = uc.astype(BF16)
    r = jax.nn.sigmoid(_block_diag(ucb, wa_ref) + ba_ref[...])
    gi = jax.nn.sigmoid(_block_diag(ucb, wi_ref) + bi_ref[...])
    nl = -lam_ref[...]
    softplus = jnp.maximum(nl, 0.0) + jnp.log1p(jnp.exp(-jnp.abs(nl)))
    log_a = (-LRU_C) * r * softplus
    a = jnp.exp(log_a)
    th = jnp.tanh(log_a)
    mult = jnp.sqrt((-2.0) * th / (1.0 - th))
    if reset_mask is not None:
        mult = jnp.where(reset_mask, 1.0, mult)
    return a, uc * gi * mult


def _rg_prompt_kernel(gu_ref, cw_ref, cb_ref, wa_ref, ba_ref, wi_ref, bi_ref, lam_ref,
                      hg_ref, cout_ref, hout_ref, tail_s, h_s, a_s, b_s, hh_s,
                      *, tt, last_tile, last_row):
    j = pl.program_id(1)

    @pl.when(j == 0)
    def _():
        tail_s[...] = jnp.zeros_like(tail_s)
        h_s[...] = jnp.zeros_like(h_s)

    gate = gu_ref[:, :D_RNN]
    u = gu_ref[:, D_RNN:]
    ext = jnp.concatenate([tail_s[...], u], axis=0)
    cw = cw_ref[...]
    uc = cb_ref[...]
    for k in range(CONV_WIDTH):
        lo = SUBLANES - (CONV_WIDTH - 1) + k
        uc = uc + cw[k:k + 1] * ext[lo:lo + tt]
    tail_s[...] = ext[tt:tt + SUBLANES]

    row = lax.broadcasted_iota(I32, (tt, 1), 0)
    a, bx = _rglru_gates(uc, wa_ref, ba_ref, wi_ref, bi_ref, lam_ref, (row + j * tt) == 0)
    a_s[...] = a
    b_s[...] = bx

    def group(g, h):
        off = pl.multiple_of(g * SUBLANES, SUBLANES)
        ag = a_s[pl.ds(off, SUBLANES), :]
        bg = b_s[pl.ds(off, SUBLANES), :]
        rows = []
        for r in range(SUBLANES):
            h = ag[r:r + 1] * h + bg[r:r + 1]
            rows.append(h)
        hh_s[pl.ds(off, SUBLANES), :] = jnp.concatenate(rows, axis=0)
        return h

    h_s[...] = lax.fori_loop(0, tt // SUBLANES, group, h_s[...])
    hh = hh_s[...]
    hg_ref[...] = (hh * _gelu_tanh(gate)).astype(BF16)

    @pl.when(j == last_tile)
    def _():
        lo = SUBLANES + last_row - (CONV_WIDTH - 2)
        cout_ref[0] = ext[lo:lo + CONV_WIDTH - 1]
        hout_ref[0] = hh[last_row:last_row + 1]


def rg_prompt(gu, n_batch, lp, l_real, cw, cb, wa, ba, wi, bi, lam, tt=128):
    n_t = lp // tt
    last_tile, last_row = (l_real - 1) // tt, (l_real - 1) % tt
    assert last_row >= CONV_WIDTH - 2
    vec = lambda r: pl.BlockSpec((r, D_RNN), lambda b, j: (0, 0))
    blk = pl.BlockSpec((RG_BLOCKS, RG_BLK, RG_BLK), lambda b, j: (0, 0, 0))
    kern = functools.partial(_rg_prompt_kernel, tt=tt, last_tile=last_tile, last_row=last_row)
    return pl.pallas_call(
        kern,
        grid=(n_batch, n_t),
        in_specs=[pl.BlockSpec((tt, 2 * D_RNN), lambda b, j: (b * n_t + j, 0)),
                  vec(CONV_WIDTH), vec(1), blk, vec(1), blk, vec(1), vec(1)],
        out_specs=[pl.BlockSpec((tt, D_RNN), lambda b, j: (b * n_t + j, 0)),
                   pl.BlockSpec((1, CONV_WIDTH - 1, D_RNN), lambda b, j: (b, 0, 0)),
                   pl.BlockSpec((1, 1, D_RNN), lambda b, j: (b, 0, 0))],
        out_shape=[jax.ShapeDtypeStruct((n_batch * lp, D_RNN), BF16),
                   jax.ShapeDtypeStruct((n_batch, CONV_WIDTH - 1, D_RNN), F32),
                   jax.ShapeDtypeStruct((n_batch, 1, D_RNN), F32)],
        scratch_shapes=[pltpu.VMEM((SUBLANES, D_RNN), F32), pltpu.VMEM((1, D_RNN), F32),
                        pltpu.VMEM((tt, D_RNN), F32), pltpu.VMEM((tt, D_RNN), F32),
                        pltpu.VMEM((tt, D_RNN), F32)],
        compiler_params=_cparams(("parallel", "arbitrary")),
        name="rg_prompt",
    )(gu, cw, cb.reshape(1, -1), wa, ba.reshape(1, -1), wi, bi.reshape(1, -1), lam.reshape(1, -1))


def _rg_sample_kernel(gu_ref, buf_ref, h0_ref, cw_ref, cb_ref, wa_ref, ba_ref, wi_ref, bi_ref, lam_ref,
                      hg_ref, cout_ref, hout_ref, *, n_t):
    cw = cw_ref[...]
    hist = [buf_ref[k] for k in range(CONV_WIDTH - 1)]
    h = h0_ref[...]
    for t in range(n_t):
        gu = gu_ref[t]
        hist.append(gu[:, D_RNN:])
        uc = cb_ref[...]
        for k in range(CONV_WIDTH):
            uc = uc + cw[k:k + 1] * hist[t + k]
        a, bx = _rglru_gates(uc, wa_ref, ba_ref, wi_ref, bi_ref, lam_ref, None)
        h = a * h + bx
        hg_ref[t] = (h * _gelu_tanh(gu[:, :D_RNN])).astype(BF16)
    for k in range(CONV_WIDTH - 1):
        cout_ref[k] = hist[n_t + k]
    hout_ref[...] = h


def rg_sample(gu_tm, buf_tm, h0, cw, cb, wa, ba, wi, bi, lam):
    n_t, nb, _ = gu_tm.shape
    kern = functools.partial(_rg_sample_kernel, n_t=n_t)
    return pl.pallas_call(
        kern,
        out_shape=[jax.ShapeDtypeStruct((n_t, nb, D_RNN), BF16),
                   jax.ShapeDtypeStruct((CONV_WIDTH - 1, nb, D_RNN), F32),
                   jax.ShapeDtypeStruct((nb, D_RNN), F32)],
        compiler_params=pltpu.CompilerParams(vmem_limit_bytes=VMEM_LIMIT),
        name="rg_sample",
    )(gu_tm, buf_tm, h0, cw, cb.reshape(1, -1), wa, ba.reshape(1, -1), wi, bi.reshape(1, -1), lam.reshape(1, -1))


def _sort_key(s):
    b = pltpu.bitcast(s, I32)
    return jnp.where(b >= 0, b, b ^ jnp.int32(0x7FFFFFFF))


def _count_ge(keys, cand):
    return jnp.sum(jnp.where(keys >= cand, 1.0, 0.0), axis=1, keepdims=True)


def _kth_largest_key(key_ref, k):
    kf = float(k)
    rows = key_ref.shape[0]
    t0 = jnp.where(_count_ge(key_ref[...], jnp.zeros((rows, 1), I32)) >= kf,
                   jnp.zeros((rows, 1), I32), jnp.full((rows, 1), INT_MIN, I32))

    def body(it, t):
        cand = t | jnp.left_shift(jnp.int32(1), jnp.int32(30) - it)
        return jnp.where(_count_ge(key_ref[...], cand) >= kf, cand, t)

    return lax.fori_loop(0, 31, body, t0)


def _selection_bias(key_ref, bias_ref, k):
    rows, cols = key_ref.shape
    t = _kth_largest_key(key_ref, k)
    keys = key_ref[...]
    need = float(k) - jnp.sum(jnp.where(keys > t, 1.0, 0.0), axis=1, keepdims=True)
    rj = lax.broadcasted_iota(I32, (LANES, LANES), 0)
    cj = lax.broadcasted_iota(I32, (LANES, LANES), 1)
    tri = jnp.where(rj < cj, 1.0, 0.0).astype(BF16)
    carry = jnp.zeros((rows, 1), F32)
    for c in range(cols // LANES):
        kc = key_ref[:, c * LANES:(c + 1) * LANES]
        eq = jnp.where(kc == t, 1.0, 0.0)
        before = jnp.dot(eq.astype(BF16), tri, preferred_element_type=F32) + carry
        take = jnp.where(kc > t, 1.0, jnp.where(before < need, eq, 0.0))
        take = jnp.where(kc == jnp.int32(KEY_NEG_INF), 0.0, take)
        bias_ref[:, c * LANES:(c + 1) * LANES] = jnp.where(take > 0.5, 0.0, NEG_BIG)
        carry = carry + jnp.sum(eq, axis=1, keepdims=True)


_NT = (((1,), (1,)), ((), ()))


def _key_chunks(lp, size=512):
    return [(c0, min(size, lp - c0)) for c0 in range(0, lp, size)]


def _attn_prompt_kernel(qi_ref, kiw_ref, kib_ref, q_ref, kb_ref, vb_ref, o_ref, key_s, bias_s,
                        *, l_real, topk):
    tq = q_ref.shape[0]
    lp = kb_ref.shape[0]
    i = pl.program_id(1)
    chunks = _key_chunks(lp)
    wi = kiw_ref[:, IDX_DIM:IDX_DIM + IDX_HEADS] * (IDX_HEADS ** -0.5)
    qpos = i * tq + lax.broadcasted_iota(I32, (tq, 1), 0)

    for c0, cs in chunks:
        kik = kib_ref[c0:c0 + cs, 0:IDX_DIM]
        acc = jnp.zeros((tq, cs), F32)
        for h in range(IDX_HEADS):
            d = lax.dot_general(qi_ref[:, h * IDX_DIM:(h + 1) * IDX_DIM], kik, _NT, preferred_element_type=F32)
            acc = acc + jnp.maximum(d, 0.0) * wi[:, h:h + 1]
        kpos = c0 + lax.broadcasted_iota(I32, (1, cs), 1)
        valid = (kpos <= qpos) & (kpos < l_real)
        key_s[:, c0:c0 + cs] = _sort_key(jnp.where(valid, acc, -jnp.inf))

    _selection_bias(key_s, bias_s, topk)

    for n in range(N_KV_HEADS):
        qs = jnp.concatenate(
            [q_ref[:, (KV_GROUP * n + g) * HEAD_DIM:(KV_GROUP * n + g + 1) * HEAD_DIM] for g in range(KV_GROUP)],
            axis=0)
        m = jnp.full((KV_GROUP * tq, 1), NEG_BIG, F32)
        l = jnp.zeros((KV_GROUP * tq, 1), F32)
        acc = jnp.zeros((KV_GROUP * tq, HEAD_DIM), F32)
        for c0, cs in chunks:
            kc = kb_ref[c0:c0 + cs, n * HEAD_DIM:(n + 1) * HEAD_DIM]
            vc = vb_ref[c0:c0 + cs, n * HEAD_DIM:(n + 1) * HEAD_DIM]
            bias = bias_s[:, c0:c0 + cs]
            lg = lax.dot_general(qs, kc, _NT, preferred_element_type=F32) + jnp.concatenate([bias] * KV_GROUP, axis=0)
            m_new = jnp.maximum(m, jnp.max(lg, axis=1, keepdims=True))
            alpha = jnp.exp(m - m_new)
            p = jnp.exp(lg - m_new)
            l = alpha * l + jnp.sum(p, axis=1, keepdims=True)
            acc = alpha * acc + jnp.dot(p.astype(BF16), vc, preferred_element_type=F32)
            m = m_new
        o = acc / l
        for g in range(KV_GROUP):
            h = KV_GROUP * n + g
            o_ref[:, h * HEAD_DIM:(h + 1) * HEAD_DIM] = o[g * tq:(g + 1) * tq].astype(BF16)


def attn_prompt(qi, kiw, kib, q, kb, vb, n_batch, lp, l_real, topk, tq=128):
    n_q = lp // tq
    qrow = lambda width: pl.BlockSpec((tq, width), lambda b, i: (b * n_q + i, 0))
    seq = lambda width: pl.BlockSpec((lp, width), lambda b, i: (b, 0))
    kern = functools.partial(_attn_prompt_kernel, l_real=l_real, topk=topk)
    return pl.pallas_call(
        kern,
        grid=(n_batch, n_q),
        in_specs=[qrow(QI_COLS), qrow(LANES), seq(LANES), qrow(Q_COLS), seq(KV_COLS), seq(KV_COLS)],
        out_specs=qrow(Q_COLS),
        out_shape=jax.ShapeDtypeStruct((n_batch * lp, Q_COLS), BF16),
        scratch_shapes=[pltpu.VMEM((tq, lp), I32), pltpu.VMEM((tq, lp), F32)],
        compiler_params=_cparams(("parallel", "parallel")),
        name="attn_prompt",
    )(qi, kiw, kib, q, kb, vb)


SAMPLE_ROWS = SUBLANES


def _attn_sample_kernel(pt_ref, qi_ref, wi_ref, kin_ref, q_ref, kn_ref, vn_ref,
                        cik_hbm, ck_hbm, cv_hbm, o_ref,
                        ki_buf, k_buf, v_buf, key_s, bias_s, sem,
                        *, layer, n_pages, n_new, topk, chunk_keys):
    b = pl.program_id(0)
    past = n_pages * PAGE_SIZE
    rows = SAMPLE_ROWS

    def page_copies(p):
        page = pt_ref[b, p]
        return (
            pltpu.make_async_copy(cik_hbm.at[layer, page], ki_buf.at[pl.ds(p * PAGE_SIZE, PAGE_SIZE)], sem.at[0]),
            pltpu.make_async_copy(ck_hbm.at[layer, page],
                                  k_buf.at[pl.ds(p * PAGE_SIZE * N_KV_HEADS, PAGE_SIZE * N_KV_HEADS)], sem.at[1]),
            pltpu.make_async_copy(cv_hbm.at[layer, page],
                                  v_buf.at[pl.ds(p * PAGE_SIZE * N_KV_HEADS, PAGE_SIZE * N_KV_HEADS)], sem.at[2]),
        )

    def start(p, carry):
        for cp in page_copies(p):
            cp.start()
        return carry

    def wait(p, carry):
        for cp in page_copies(p):
            cp.wait()
        return carry

    lax.fori_loop(0, n_pages, start, 0)
    lax.fori_loop(0, n_pages, wait, 0)

    qi = qi_ref[0]
    wi = wi_ref[0][:, 0:1]

    def head_sum(x):
        tot = x[0:rows]
        for h in range(1, IDX_HEADS):
            tot = tot + x[h * rows:(h + 1) * rows]
        return tot

    qrow = lax.broadcasted_iota(I32, (rows, 1), 0)
    for c in range(past // chunk_keys):
        kik = ki_buf[c * chunk_keys:(c + 1) * chunk_keys, :].astype(BF16)
        d = lax.dot_general(qi, kik, _NT, preferred_element_type=F32)
        s = head_sum(jnp.maximum(d, 0.0) * wi)
        key_s[:, c * chunk_keys:(c + 1) * chunk_keys] = _sort_key(s)
    d = lax.dot_general(qi, kin_ref[0][:, 0:IDX_DIM], _NT, preferred_element_type=F32)
    s = head_sum(jnp.maximum(d, 0.0) * wi)
    col = lax.broadcasted_iota(I32, (1, LANES), 1)
    valid = (col <= qrow) & (col < n_new)
    key_s[:, past:past + LANES] = _sort_key(jnp.where(valid, s, -jnp.inf))

    _selection_bias(key_s, bias_s, topk)

    for n in range(N_KV_HEADS):
        qs = q_ref[0, n]
        m = jnp.full((KV_GROUP * rows, 1), NEG_BIG, F32)
        l = jnp.zeros((KV_GROUP * rows, 1), F32)
        acc = jnp.zeros((KV_GROUP * rows, HEAD_DIM), F32)

        def step(lg, vc, m, l, acc):
            m_new = jnp.maximum(m, jnp.max(lg, axis=1, keepdims=True))
            alpha = jnp.exp(m - m_new)
            p = jnp.exp(lg - m_new)
            l = alpha * l + jnp.sum(p, axis=1, keepdims=True)
            acc = alpha * acc + jnp.dot(p.astype(BF16), vc, preferred_element_type=F32)
            return m_new, l, acc

        for c in range(past // chunk_keys):
            sl = pl.ds(c * chunk_keys * N_KV_HEADS + n, chunk_keys, stride=N_KV_HEADS)
            kc = k_buf[sl, :].astype(BF16)
            vc = v_buf[sl, :].astype(BF16)
            bias = bias_s[:, c * chunk_keys:(c + 1) * chunk_keys]
            lg = lax.dot_general(qs, kc, _NT, preferred_element_type=F32) + jnp.concatenate([bias] * KV_GROUP, axis=0)
            m, l, acc = step(lg, vc, m, l, acc)
        kc = kn_ref[0][:, n * HEAD_DIM:(n + 1) * HEAD_DIM]
        vc = vn_ref[0][:, n * HEAD_DIM:(n + 1) * HEAD_DIM]
        bias = bias_s[:, past:past + LANES]
        lg = lax.dot_general(qs, kc, _NT, preferred_element_type=F32) + jnp.concatenate([bias] * KV_GROUP, axis=0)
        m, l, acc = step(lg, vc, m, l, acc)
        o = acc / l
        for g in range(KV_GROUP):
            h = KV_GROUP * n + g
            o_ref[0, :, h * HEAD_DIM:(h + 1) * HEAD_DIM] = o[g * rows:(g + 1) * rows]


def attn_sample(page_table, qi_s, wi_s, kin_s, q_s, kn_s, vn_s, cache_idx_k, cache_k, cache_v,
                layer, n_new, topk, chunk_keys=1024):
    nb, n_pages = page_table.shape
    past = n_pages * PAGE_SIZE
    cols = past + LANES
    per_b = lambda shape: pl.BlockSpec((1,) + shape, lambda b, pt: (b,) + (0,) * len(shape))
    kern = functools.partial(_attn_sample_kernel, layer=layer, n_pages=n_pages, n_new=n_new, topk=topk,
                             chunk_keys=chunk_keys)
    grid_spec = pltpu.PrefetchScalarGridSpec(
        num_scalar_prefetch=1,
        grid=(nb,),
        in_specs=[per_b((IDX_HEADS * SAMPLE_ROWS, IDX_DIM)), per_b((IDX_HEADS * SAMPLE_ROWS, LANES)),
                  per_b((LANES, LANES)), per_b((N_KV_HEADS, KV_GROUP * SAMPLE_ROWS, HEAD_DIM)),
                  per_b((LANES, KV_COLS)), per_b((LANES, KV_COLS)),
                  pl.BlockSpec(memory_space=pl.ANY), pl.BlockSpec(memory_space=pl.ANY),
                  pl.BlockSpec(memory_space=pl.ANY)],
        out_specs=per_b((SAMPLE_ROWS, Q_COLS)),
        scratch_shapes=[pltpu.VMEM((past, IDX_DIM), F32),
                        pltpu.VMEM((past * N_KV_HEADS, HEAD_DIM), F32),
                        pltpu.VMEM((past * N_KV_HEADS, HEAD_DIM), F32),
                        pltpu.VMEM((SAMPLE_ROWS, cols), I32), pltpu.VMEM((SAMPLE_ROWS, cols), F32),
                        pltpu.SemaphoreType.DMA((3,))])
    return pl.pallas_call(
        kern,
        grid_spec=grid_spec,
        out_shape=jax.ShapeDtypeStruct((nb, SAMPLE_ROWS, Q_COLS), F32),
        compiler_params=_cparams(("arbitrary",)),
        name="attn_sample",
    )(page_table, qi_s, wi_s, kin_s, q_s, kn_s, vn_s, cache_idx_k, cache_k, cache_v)


ROUTER_TM = 512
MOE_CHUNK = 2560
MOE_TILE = 128
MOE_FLUSH = 512


def _router_kernel(x_ref, valid_ref, w_ref, b_ref, idx_ref, gate_ref, rank_ref, cnt_ref, carry_s,
                   *, tiles_per_chunk):
    i = pl.program_id(0)
    tm = x_ref.shape[0]

    @pl.when(i % tiles_per_chunk == 0)
    def _():
        carry_s[...] = jnp.zeros_like(carry_s)

    logits = jnp.dot(x_ref[...], w_ref[...], preferred_element_type=F32,
                     precision=lax.Precision.HIGHEST) + b_ref[...]
    lane = lax.broadcasted_iota(I32, (tm, N_EXPERTS), 1)
    l = logits
    vals, idxs = [], []
    for _ in range(TOP_K):
        mx = jnp.max(l, axis=1, keepdims=True)
        ix = jnp.min(jnp.where(l == mx, lane, N_EXPERTS), axis=1, keepdims=True)
        vals.append(mx)
        idxs.append(ix)
        l = jnp.where(lane == ix, -jnp.inf, l)
    es = [jnp.exp(v - vals[0]) for v in vals]
    tot = es[0] + es[1] + es[2] + es[3]

    valid = valid_ref[...]
    onehot = jnp.zeros((tm, N_EXPERTS), F32)
    for ix in idxs:
        onehot = onehot + jnp.where(lane == ix, 1.0, 0.0)
    onehot = onehot * valid
    rj = lax.broadcasted_iota(I32, (tm, tm), 0)
    cj = lax.broadcasted_iota(I32, (tm, tm), 1)
    tri = jnp.where(cj < rj, 1.0, 0.0).astype(BF16)
    before = jnp.dot(tri, onehot.astype(BF16), preferred_element_type=F32) + carry_s[...]
    carry = carry_s[...] + jnp.sum(onehot, axis=0, keepdims=True)
    carry_s[...] = carry

    out_lane = lax.broadcasted_iota(I32, (tm, LANES), 1)
    idx_out = jnp.zeros((tm, LANES), I32)
    gate_out = jnp.zeros((tm, LANES), F32)
    rank_out = jnp.zeros((tm, LANES), I32)
    for k in range(TOP_K):
        rk = jnp.sum(jnp.where(lane == idxs[k], before, 0.0), axis=1, keepdims=True)
        idx_out = jnp.where(out_lane == k, idxs[k], idx_out)
        gate_out = jnp.where(out_lane == k, es[k] / tot, gate_out)
        rank_out = jnp.where(out_lane == k, rk.astype(I32), rank_out)
    idx_ref[...] = idx_out
    gate_ref[...] = gate_out
    rank_ref[...] = rank_out
    cnt_ref[0] = jnp.concatenate(
        [jnp.broadcast_to(carry, (SUBLANES, N_EXPERTS)), jnp.zeros((SUBLANES, LANES - N_EXPERTS), F32)],
        axis=1).astype(I32)


def moe_router(x, valid, w_r, b_r):
    n = x.shape[0]
    tm = ROUTER_TM
    tiles_per_chunk = MOE_CHUNK // tm
    n_chunks = n // MOE_CHUNK
    row = pl.BlockSpec((tm, LANES), lambda i: (i, 0))
    kern = functools.partial(_router_kernel, tiles_per_chunk=tiles_per_chunk)
    return pl.pallas_call(
        kern,
        grid=(n // tm,),
        in_specs=[pl.BlockSpec((tm, D_MODEL), lambda i: (i, 0)),
                  pl.BlockSpec((tm, 1), lambda i: (i, 0)),
                  pl.BlockSpec((D_MODEL, N_EXPERTS), lambda i: (0, 0)),
                  pl.BlockSpec((1, N_EXPERTS), lambda i: (0, 0))],
        out_specs=[row, row, row,
                   pl.BlockSpec((1, SUBLANES, LANES), lambda i: (i // tiles_per_chunk, 0, 0))],
        out_shape=[jax.ShapeDtypeStruct((n, LANES), I32), jax.ShapeDtypeStruct((n, LANES), F32),
                   jax.ShapeDtypeStruct((n, LANES), I32),
                   jax.ShapeDtypeStruct((n_chunks, SUBLANES, LANES), I32)],
        scratch_shapes=[pltpu.VMEM((1, N_EXPERTS), F32)],
        compiler_params=_cparams(("arbitrary",)),
        name="moe_router",
    )(x, valid, w_r, b_r.reshape(1, N_EXPERTS))


def _moe_kernel(te_ref, nt_ref, dest_ref, gate_ref, x_ref, wgu_ref, bgu_ref, wdn_ref, bdn_ref, g_ref, b_ref,
                o_ref, tok_s, gl_s, xs_s, ys_s, yacc_s, *, max_tiles, slots):
    c = pl.program_id(0)
    j = pl.program_id(1)
    n_assign = dest_ref.shape[0]

    @pl.when(j == 0)
    def _():
        yacc_s[...] = jnp.zeros_like(yacc_s)

        def clear(p, carry):
            tok_s[p] = 0
            gl_s[p] = 0.0
            return carry

        lax.fori_loop(0, slots, clear, 0)

        def fill(a, carry):
            d = dest_ref[a]
            tok_s[d] = a // TOP_K
            gl_s[d] = gate_ref[a]
            return carry

        lax.fori_loop(0, n_assign, fill, 0)

    @pl.when(j < nt_ref[c])
    def _():
        e = te_ref[c, j]
        base = j * MOE_TILE

        def gather(p, carry):
            t = tok_s[base + p]
            xs_s[pl.ds(p, 1), :] = x_ref[pl.ds(t, 1), :]
            return carry

        lax.fori_loop(0, MOE_TILE, gather, 0)
        h = jnp.dot(xs_s[...].astype(BF16), wgu_ref[0], preferred_element_type=F32) + bgu_ref[pl.ds(e, 1), :]
        glu = jnp.minimum(h[:, :D_FF], SWIGLU_LIMIT)
        lin = jnp.clip(h[:, D_FF:], -SWIGLU_LIMIT, SWIGLU_LIMIT)
        act = glu * jax.nn.sigmoid(SWIGLU_ALPHA * glu) * (lin + 1.0)
        ys_s[...] = jnp.dot(act.astype(BF16), wdn_ref[0], preferred_element_type=F32) + bdn_ref[pl.ds(e, 1), :]

        def scatter(p, carry):
            t = tok_s[base + p]
            g = gl_s[base + p]
            yacc_s[pl.ds(t, 1), :] = yacc_s[pl.ds(t, 1), :] + g * ys_s[pl.ds(p, 1), :]
            return carry

        lax.fori_loop(0, MOE_TILE, scatter, 0)

    @pl.when(j >= max_tiles)
    def _():
        r0 = pl.multiple_of((j - max_tiles) * MOE_FLUSH, MOE_FLUSH)
        z = DN_ALPHA * x_ref[pl.ds(r0, MOE_FLUSH), :] + yacc_s[pl.ds(r0, MOE_FLUSH), :]
        o_ref[...] = _layer_norm(z, g_ref[...], b_ref[...])


def moe_ffn_ln(x, tile_expert, n_tiles, dest, gates, w_gu, b_gu, w_dn, b_dn, g, b, max_tiles, slots):
    n = x.shape[0]
    n_chunks = n // MOE_CHUNK
    n_flush = MOE_CHUNK // MOE_FLUSH
    n_assign = MOE_CHUNK * TOP_K

    def expert_of(c, j, te, nt):
        return te[c, jnp.minimum(j, jnp.maximum(nt[c] - 1, 0))]

    grid_spec = pltpu.PrefetchScalarGridSpec(
        num_scalar_prefetch=2,
        grid=(n_chunks, max_tiles + n_flush),
        in_specs=[
            pl.BlockSpec((n_assign,), lambda c, j, te, nt: (c,), memory_space=pltpu.SMEM),
            pl.BlockSpec((n_assign,), lambda c, j, te, nt: (c,), memory_space=pltpu.SMEM),
            pl.BlockSpec((MOE_CHUNK, D_MODEL), lambda c, j, te, nt: (c, 0), pipeline_mode=pl.Buffered(1)),
            pl.BlockSpec((1, D_MODEL, 2 * D_FF), lambda c, j, te, nt: (expert_of(c, j, te, nt), 0, 0)),
            pl.BlockSpec((N_EXPERTS, 2 * D_FF), lambda c, j, te, nt: (0, 0)),
            pl.BlockSpec((1, D_FF, D_MODEL), lambda c, j, te, nt: (expert_of(c, j, te, nt), 0, 0)),
            pl.BlockSpec((N_EXPERTS, D_MODEL), lambda c, j, te, nt: (0, 0)),
            pl.BlockSpec((1, D_MODEL), lambda c, j, te, nt: (0, 0)),
            pl.BlockSpec((1, D_MODEL), lambda c, j, te, nt: (0, 0)),
        ],
        out_specs=pl.BlockSpec((MOE_FLUSH, D_MODEL),
                               lambda c, j, te, nt: (c * n_flush + jnp.maximum(j - max_tiles, 0), 0)),
        scratch_shapes=[pltpu.SMEM((slots + SUBLANES,), I32), pltpu.SMEM((slots + SUBLANES,), F32),
                        pltpu.VMEM((MOE_TILE, D_MODEL), F32), pltpu.VMEM((MOE_TILE, D_MODEL), F32),
                        pltpu.VMEM((MOE_CHUNK, D_MODEL), F32)])
    kern = functools.partial(_moe_kernel, max_tiles=max_tiles, slots=slots)
    return pl.pallas_call(
        kern,
        grid_spec=grid_spec,
        out_shape=jax.ShapeDtypeStruct((n, D_MODEL), F32),
        compiler_params=_cparams(("arbitrary", "arbitrary")),
        name="moe_ffn_ln",
    )(tile_expert, n_tiles, dest, gates, x, w_gu, b_gu, w_dn, b_dn, g.reshape(1, -1), b.reshape(1, -1))


def moe_layer(x, valid, w_r, b_r, w_gu, b_gu, w_dn, b_dn, g, b):
    n = x.shape[0]
    n_chunks = n // MOE_CHUNK
    slots = MOE_CHUNK * TOP_K + N_EXPERTS * (MOE_TILE - 1)
    max_tiles = -(-slots // MOE_TILE)
    slots = max_tiles * MOE_TILE

    idx, gate, rank, cnt = moe_router(x, valid, w_r, b_r)
    idx, gate, rank = idx[:, :TOP_K], gate[:, :TOP_K], rank[:, :TOP_K]
    counts = cnt[:, 0, :N_EXPERTS]
    tiles = (counts + MOE_TILE - 1) // MOE_TILE
    tile_end = jnp.cumsum(tiles, axis=1)
    base = (tile_end - tiles) * MOE_TILE
    n_tiles = tile_end[:, -1].astype(I32)
    tile_expert = jnp.sum(jnp.arange(max_tiles, dtype=I32)[None, :, None] >= tile_end[:, None, :], axis=2)
    tile_expert = jnp.minimum(tile_expert, N_EXPERTS - 1).astype(I32)
    chunk_of = jnp.arange(n, dtype=I32) // MOE_CHUNK
    onehot = idx[:, :, None] == jnp.arange(N_EXPERTS, dtype=I32)[None, None, :]
    dest = jnp.sum(jnp.where(onehot, base[chunk_of][:, None, :], 0), axis=2) + rank
    dest = jnp.where(valid > 0.5, dest, slots).astype(I32)
    return moe_ffn_ln(x, tile_expert, n_tiles, dest.reshape(-1), gate.reshape(-1),
                      w_gu, b_gu, w_dn, b_dn, g, b, max_tiles, slots)


def _rope_tables(pos):
    half = HEAD_DIM // 2
    inv_freq = ROPE_THETA ** (-jnp.arange(half, dtype=F32) / half)
    ang = pos.astype(F32)[:, None] * inv_freq[None, :]
    cos, sin = jnp.cos(ang), jnp.sin(ang)
    c = jnp.concatenate([cos, cos, cos, cos], axis=1)
    s = jnp.concatenate([-sin, sin, -sin, sin], axis=1)
    ones = jnp.ones_like(cos)
    zeros = jnp.zeros_like(cos)
    c2 = jnp.concatenate([cos, cos, ones, ones], axis=1)
    s2 = jnp.concatenate([-sin, sin, zeros, zeros], axis=1)
    return c, s, c2, s2


def kernel(x_prompt, x_sample, cache_k, cache_v, cache_idx_k, state_conv, state_h, page_table, meta_tokens,
           w_in_rg, conv_w_rg, conv_b_rg, w_a_rg, b_a_rg, w_i_rg, b_i_rg, lam_rg, w_out_rg, w_qkv_att, w_o_att,
           ln_mix_g, ln_mix_b, w_router, b_router, w_moe_gu, b_moe_gu, w_moe_dn, b_moe_dn, ln_ffn_g, ln_ffn_b):
    nb, seq, _ = x_prompt.shape
    ns, dec, _ = x_sample.shape
    l_real = seq + N_META
    lp = _round_up(l_real, LANES)
    np_rows = nb * lp
    ns_rows = ns * dec
    n_rows = _round_up(np_rows + ns_rows, MOE_CHUNK)
    n_pages = page_table.shape[1]
    past = n_pages * PAGE_SIZE
    assert dec <= SAMPLE_ROWS and n_rows % ROUTER_TM == 0 and lp % LANES == 0

    meta = jnp.broadcast_to(meta_tokens[None], (nb, N_META, D_MODEL))
    xp = jnp.concatenate([meta, x_prompt, jnp.zeros((nb, lp - l_real, D_MODEL), F32)], axis=1)
    x = jnp.concatenate([xp.reshape(np_rows, D_MODEL), x_sample.reshape(ns_rows, D_MODEL),
                         jnp.zeros((n_rows - np_rows - ns_rows, D_MODEL), F32)], axis=0)
    r = jnp.arange(n_rows, dtype=I32)
    in_prompt = r < np_rows
    in_sample = (r >= np_rows) & (r < np_rows + ns_rows)
    pos = jnp.where(in_prompt, r % lp, jnp.where(in_sample, past + (r - np_rows) % dec, 0))
    valid = ((in_prompt & (r % lp < l_real)) | in_sample).astype(F32).reshape(n_rows, 1)
    rope_c, rope_s, rope_c2, rope_s2 = _rope_tables(pos)

    topk_p = min(TOPK_MAX, l_real // 4)
    topk_s = min(TOPK_MAX, (past + dec) // 4)

    w_in_b = w_in_rg.astype(BF16)
    w_out_b = w_out_rg.astype(BF16)
    w_a_b = w_a_rg.astype(BF16)
    w_i_b = w_i_rg.astype(BF16)
    w_qkv_b = jnp.pad(w_qkv_att, ((0, 0), (0, 0), (0, QKV_PAD - w_qkv_att.shape[2]))).astype(BF16)
    w_o_b = w_o_att.astype(BF16)
    w_gu_b = w_moe_gu.astype(BF16)
    w_dn_b = w_moe_dn.astype(BF16)
    cache_k_r = cache_k.reshape(cache_k.shape[0], cache_k.shape[1], PAGE_SIZE * N_KV_HEADS, HEAD_DIM)
    cache_v_r = cache_v.reshape(cache_v.shape[0], cache_v.shape[1], PAGE_SIZE * N_KV_HEADS, HEAD_DIM)

    def sample_rows(a):
        return a[np_rows:np_rows + ns_rows]

    def pad_rows(a, rows):
        return jnp.pad(a, [(0, 0), (0, rows - a.shape[1])] + [(0, 0)] * (a.ndim - 2))

    outs = {k: [] for k in ("kp", "vp", "ip", "cp", "hp", "ks", "vs", "is", "cs", "hs")}
    for i in range(DEPTH):
        j = i // 2
        if i % 2 == 0:
            gu = matmul(x, w_in_b[j])
            rg = (conv_w_rg[j], conv_b_rg[j], w_a_b[j], b_a_rg[j], w_i_b[j], b_i_rg[j], lam_rg[j])
            hg_p, cp, hp = rg_prompt(gu, nb, lp, l_real, *rg)
            gu_s = sample_rows(gu).reshape(ns, dec, 2 * D_RNN).transpose(1, 0, 2)
            hg_s, cs, hs = rg_sample(gu_s, state_conv[j].transpose(1, 0, 2), state_h[j], *rg)
            outs["cp"].append(cp)
            outs["hp"].append(hp[:, 0])
            outs["cs"].append(cs.transpose(1, 0, 2))
            outs["hs"].append(hs)
            mix = jnp.concatenate([hg_p, hg_s.transpose(1, 0, 2).reshape(ns_rows, D_RNN),
                                   jnp.zeros((n_rows - np_rows - ns_rows, D_RNN), BF16)], axis=0)
            w_mix = w_out_b[j]
        else:
            q, k, v, kb, vb, qi, kiw, kib = qkv_project(x, w_qkv_b[j], rope_c, rope_s, rope_c2, rope_s2)
            o_p = attn_prompt(qi, kiw, kib, q, kb, vb, nb, lp, l_real, topk_p)
            qi_s = pad_rows(sample_rows(qi).reshape(ns, dec, IDX_HEADS, IDX_DIM), SAMPLE_ROWS)
            qi_s = qi_s.transpose(0, 2, 1, 3).reshape(ns, IDX_HEADS * SAMPLE_ROWS, IDX_DIM)
            wi_s = pad_rows(sample_rows(kiw)[:, IDX_DIM:IDX_DIM + IDX_HEADS].reshape(ns, dec, IDX_HEADS), SAMPLE_ROWS)
            wi_s = wi_s.transpose(0, 2, 1).reshape(ns, IDX_HEADS * SAMPLE_ROWS, 1) * (IDX_HEADS ** -0.5)
            wi_s = jnp.broadcast_to(wi_s, (ns, IDX_HEADS * SAMPLE_ROWS, LANES))
            kin_s = pad_rows(sample_rows(kib).reshape(ns, dec, LANES), LANES)
            q_s = pad_rows(sample_rows(q).reshape(ns, dec, N_KV_HEADS, KV_GROUP, HEAD_DIM), SAMPLE_ROWS)
            q_s = q_s.transpose(0, 2, 3, 1, 4).reshape(ns, N_KV_HEADS, KV_GROUP * SAMPLE_ROWS, HEAD_DIM)
            kn_s = pad_rows(sample_rows(kb).reshape(ns, dec, KV_COLS), LANES)
            vn_s = pad_rows(sample_rows(vb).reshape(ns, dec, KV_COLS), LANES)
            o_s = attn_sample(page_table, qi_s, wi_s, kin_s, q_s, kn_s, vn_s, cache_idx_k, cache_k_r, cache_v_r,
                              j, dec, topk_s)
            kp = k[:np_rows].reshape(nb, lp, KV_COLS)[:, :l_real]
            vp = v[:np_rows].reshape(nb, lp, KV_COLS)[:, :l_real]
            outs["kp"].append(kp.reshape(nb, l_real, N_KV_HEADS, HEAD_DIM))
            outs["vp"].append(vp.reshape(nb, l_real, N_KV_HEADS, HEAD_DIM))
            outs["ip"].append(kiw[:np_rows].reshape(nb, lp, LANES)[:, :l_real, :IDX_DIM])
            outs["ks"].append(sample_rows(k).reshape(ns, dec, N_KV_HEADS, HEAD_DIM))
            outs["vs"].append(sample_rows(v).reshape(ns, dec, N_KV_HEADS, HEAD_DIM))
            outs["is"].append(sample_rows(kiw)[:, :IDX_DIM].reshape(ns, dec, IDX_DIM))
            mix = jnp.concatenate([o_p, o_s[:, :dec].reshape(ns_rows, Q_COLS).astype(BF16),
                                   jnp.zeros((n_rows - np_rows - ns_rows, Q_COLS), BF16)], axis=0)
            w_mix = w_o_b[j]
        x = matmul_residual_ln(mix, w_mix, x, ln_mix_g[i], ln_mix_b[i])
        x = moe_layer(x, valid, w_router[i], b_router[i], w_gu_b[i], b_moe_gu[i], w_dn_b[i], b_moe_dn[i],
                      ln_ffn_g[i], ln_ffn_b[i])

    y_prompt = x[:np_rows].reshape(nb, lp, D_MODEL)[:, N_META:l_real]
    y_sample = sample_rows(x).reshape(ns, dec, D_MODEL)
    st = jnp.stack
    return (y_prompt, y_sample, st(outs["kp"]), st(outs["vp"]), st(outs["ip"]), st(outs["cp"]), st(outs["hp"]),
            st(outs["ks"]), st(outs["vs"]), st(outs["is"]), st(outs["cs"]), st(outs["hs"]))
```
